```python
import math
import jax, jax.numpy as jnp
from jax import lax
import numpy as np

D_MODEL = 1024
BATCH = 8
SEQ = 8192
DEPTH = 1
DEC_BATCH = 2
DEC_SEQ = 16384
PAST_LEN = 128

ATTN_HEAD_DIM = 128
ATTN_HEADS_PER_GROUP = 4
DILATED_GROUPS = ((128, 1), (512, 4), (2048, 16))
N_ATTN_GROUPS = 3
ATTN_WIDTH = N_ATTN_GROUPS * ATTN_HEADS_PER_GROUP * ATTN_HEAD_DIM
ATTN_OUT = ATTN_HEADS_PER_GROUP * ATTN_HEAD_DIM
ROPE_THETA = 10000.0
GLA_HEADS = 4
GLA_DK = D_MODEL // 8
GLA_DV = D_MODEL // 4
GLA_DK_TOTAL = GLA_HEADS * GLA_DK
GLA_DV_TOTAL = GLA_HEADS * GLA_DV
GLA_RANK = 16
GLA_TAU = 16.0
GLA_CHUNK = 64
PEER_HEADS = 8
PEER_KEYS = 128
PEER_EXPERTS = PEER_KEYS * PEER_KEYS
PEER_QDIM = 256
PEER_HALF = PEER_QDIM // 2
PEER_TOPK = 16
PEER_BLOCK = 128
N_MOD = 6
RMS_EPS = 1e-6
NEG_INF = -1e30
IN_SPLITS = (ATTN_WIDTH, ATTN_WIDTH, ATTN_WIDTH,
             GLA_DK_TOTAL, GLA_DK_TOTAL, GLA_DV_TOTAL, GLA_DV_TOTAL,
             GLA_RANK, GLA_RANK, D_MODEL, D_MODEL)
N_IN = sum(IN_SPLITS)

kernel_name = 'hybrid_dilated_gla_peer_encoder'


def _split_points(sizes):
    pts, acc = [], 0
    for s in sizes[:-1]:
        acc += s
        pts.append(acc)
    return pts


def rms_norm(x, g):
    xf = x.astype(jnp.float32)
    y = xf * lax.rsqrt(jnp.mean(xf * xf, axis=-1, keepdims=True) + RMS_EPS)
    return (y * g.astype(jnp.float32)).astype(x.dtype)


def rope(t):
    S, E = t.shape[1], t.shape[-1]
    half = E // 2
    inv = ROPE_THETA ** (-jnp.arange(half, dtype=jnp.float32) / half)
    ang = jnp.arange(S, dtype=jnp.float32)[:, None] * inv[None, :]
    cos = jnp.cos(ang)[None, :, None, :].astype(t.dtype)
    sin = jnp.sin(ang)[None, :, None, :].astype(t.dtype)
    t1, t2 = t[..., :half], t[..., half:]
    return jnp.concatenate([t1 * cos - t2 * sin, t2 * cos + t1 * sin], axis=-1)


def dilated_band_attention(q, k, v, dil, half):
    B, S, H, E = q.shape
    L = half
    span = dil * L
    S_pad = -(-S // span) * span
    M = S_pad // dil
    nb = M // L
    pad = ((0, 0), (0, S_pad - S), (0, 0), (0, 0))

    def to_res(t):
        return jnp.pad(t, pad).reshape(B, M, dil, H, E).transpose(0, 2, 3, 1, 4)

    def band(t):
        tp = jnp.pad(t, ((0, 0), (0, 0), (0, 0), (L, L), (0, 0))).reshape(B, dil, H, nb + 2, L, E)
        return jnp.concatenate([tp[:, :, :, :-2], tp[:, :, :, 1:-1], tp[:, :, :, 2:]], axis=-2)

    qb = to_res(q).reshape(B, dil, H, nb, L, E)
    kb = band(to_res(k))
    vb = band(to_res(v))
    m_idx = jnp.arange(-L, M + L)
    orig = m_idx[None, :] * dil + jnp.arange(dil)[:, None]
    kvalid = ((m_idx >= 0)[None, :] & (orig < S)).reshape(dil, nb + 2, L)
    kvalid = jnp.concatenate([kvalid[:, :-2], kvalid[:, 1:-1], kvalid[:, 2:]], axis=-1)
    rel = jnp.arange(3 * L)[None, :] - L - jnp.arange(L)[:, None]
    mask = (jnp.abs(rel) <= half)[None, None] & kvalid[:, :, None, :]
    s = jnp.einsum('brhnqe,brhnke->brhnqk', qb, kb, preferred_element_type=jnp.float32)
    s = jnp.where(mask[None, :, None], s, NEG_INF)
    lse = jax.nn.logsumexp(s, axis=-1)
    p = jnp.exp(s - lse[..., None]).astype(v.dtype)
    o = jnp.einsum('brhnqk,brhnke->brhnqe', p, vb)
    o = o.reshape(B, dil, H, M, E).transpose(0, 3, 1, 2, 4).reshape(B, S_pad, H, E)[:, :S]
    lse = lse.reshape(B, dil, H, M).transpose(0, 3, 1, 2).reshape(B, S_pad, H)[:, :S]
    return o, lse


def gla_chunked(q, k, v, log_a):
    B, S, H, K = q.shape
    V = v.shape[-1]
    n = S // GLA_CHUNK

    def chunks(t):
        return t.reshape(B, n, GLA_CHUNK, H, t.shape[-1]).transpose(1, 0, 3, 2, 4).astype(jnp.float32)

    qc, kc, vc, ac = chunks(q), chunks(k), chunks(v), chunks(log_a)
    b = jnp.cumsum(ac, axis=3)
    b_last = b[:, :, :, -1:]
    qd = qc * jnp.exp(b)
    kd = kc * jnp.exp(-b)
    kl = kc * jnp.exp(b_last - b)
    tri = jnp.tril(jnp.ones((GLA_CHUNK, GLA_CHUNK), bool))
    att = jnp.where(tri, jnp.einsum('nbhik,nbhjk->nbhij', qd, kd), 0.0)
    o_intra = jnp.einsum('nbhij,nbhjv->nbhiv', att, vc)

    def step(state, xs):
        qd_n, kl_n, v_n, dec_n = xs
        o_n = jnp.einsum('bhik,bhkv->bhiv', qd_n, state)
        state = dec_n[..., None] * state + jnp.einsum('bhjk,bhjv->bhkv', kl_n, v_n)
        return state, o_n

    init = jnp.zeros((B, H, K, V), jnp.float32)
    _, o_inter = lax.scan(step, init, (qd, kl, vc, jnp.exp(b_last[:, :, :, 0])))
    o = (o_intra + o_inter).transpose(1, 0, 3, 2, 4).reshape(B, S, H, V)
    return o.astype(v.dtype)


def peer_layer(h, w_q, subkeys, u_tab, v_tab):
    T, D = h.shape
    hb = h.reshape(T // PEER_BLOCK, PEER_BLOCK, D)

    def one(xb):
        q = (xb @ w_q).reshape(PEER_BLOCK, PEER_HEADS, 2, PEER_HALF)
        s = jnp.einsum('thpe,pke->thpk', q, subkeys, preferred_element_type=jnp.float32)
        sv, si = lax.top_k(s, PEER_TOPK)
        cand = (sv[:, :, 0, :, None] + sv[:, :, 1, None, :]).reshape(PEER_BLOCK, PEER_HEADS, PEER_TOPK * PEER_TOPK)
        cv, ci = lax.top_k(cand, PEER_TOPK)
        i1 = jnp.take_along_axis(si[:, :, 0], ci // PEER_TOPK, axis=-1)
        i2 = jnp.take_along_axis(si[:, :, 1], ci % PEER_TOPK, axis=-1)
        eid = (i1 * PEER_KEYS + i2).reshape(PEER_BLOCK, PEER_HEADS * PEER_TOPK)
        g = jax.nn.softmax(cv, axis=-1).reshape(PEER_BLOCK, PEER_HEADS * PEER_TOPK)
        u = jnp.take(u_tab, eid, axis=0)
        a = jnp.einsum('tkd,td->tk', u, xb, preferred_element_type=jnp.float32)
        act = (jax.nn.gelu(a, approximate=False) * g).astype(xb.dtype)
        return jnp.einsum('tk,tkd->td', act, jnp.take(v_tab, eid, axis=0))

    return lax.map(one, hb).reshape(T, D)


def encoder_layer(x, c, w_ada, b_ada, g_pre_mix, g_post_mix, w_in, w_gla_up_fwd, b_gla_fwd,
                  w_gla_up_bwd, b_gla_bwd, g_gla_norm, w_branch_a, w_branch_b, w_out,
                  g_pre_ffn, g_post_ffn, w_peer_q, peer_subkeys, peer_u, peer_v):
    B, S, D = x.shape
    mod = jax.nn.silu(c) @ w_ada + b_ada
    sh1, sc1, gt1, sh2, sc2, gt2 = [m[:, None, :] for m in jnp.split(mod, N_MOD, axis=-1)]

    h = rms_norm(x, g_pre_mix) * (1 + sc1) + sh1
    aq, ak, av, gq, gk, gv, gr, lr_f, lr_b, m_a, m_b = jnp.split(h @ w_in, _split_points(IN_SPLITS), axis=-1)

    heads = (B, S, N_ATTN_GROUPS * ATTN_HEADS_PER_GROUP, ATTN_HEAD_DIM)
    grouped = (B, S, N_ATTN_GROUPS, ATTN_HEADS_PER_GROUP, ATTN_HEAD_DIM)
    aq = (rope(aq.reshape(heads)) * ATTN_HEAD_DIM ** -0.5).reshape(grouped)
    ak = rope(ak.reshape(heads)).reshape(grouped)
    av = av.reshape(grouped)
    outs, lses = [], []
    for gi, (win, dil) in enumerate(DILATED_GROUPS):
        o_g, l_g = dilated_band_attention(aq[:, :, gi], ak[:, :, gi], av[:, :, gi], dil, win // (2 * dil))
        outs.append(o_g)
        lses.append(l_g)
    wts = jax.nn.softmax(jnp.stack(lses), axis=0)
    a_out = jnp.einsum('gbsh,gbshe->bshe', wts, jnp.stack(outs).astype(jnp.float32))
    a_out = a_out.astype(x.dtype).reshape(B, S, ATTN_OUT)

    gq = gq.reshape(B, S, GLA_HEADS, GLA_DK) * GLA_DK ** -0.5
    gk = gk.reshape(B, S, GLA_HEADS, GLA_DK)
    gv = gv.reshape(B, S, GLA_HEADS, GLA_DV)
    la_f = (jax.nn.log_sigmoid((lr_f @ w_gla_up_fwd + b_gla_fwd).astype(jnp.float32)) / GLA_TAU).reshape(B, S, GLA_HEADS, GLA_DK)
    la_b = (jax.nn.log_sigmoid((lr_b @ w_gla_up_bwd + b_gla_bwd).astype(jnp.float32)) / GLA_TAU).reshape(B, S, GLA_HEADS, GLA_DK)
    flip = lambda t: jnp.flip(t, axis=1)
    go = gla_chunked(gq, gk, gv, la_f) + flip(gla_chunked(flip(gq), flip(gk), flip(gv), flip(la_b)))
    go = rms_norm(go, g_gla_norm.reshape(GLA_HEADS, GLA_DV)).reshape(B, S, GLA_DV_TOTAL) * jax.nn.silu(gr)

    merged = jax.nn.sigmoid(m_a) * (a_out @ w_branch_a) + jax.nn.sigmoid(m_b) * (go @ w_branch_b)
    x = x + gt1 * rms_norm(merged @ w_out, g_post_mix)

    h2 = rms_norm(x, g_pre_ffn) * (1 + sc2) + sh2
    f = peer_layer(h2.reshape(B * S, D), w_peer_q, peer_subkeys, peer_u, peer_v).reshape(B, S, D)
    return x + gt2 * rms_norm(f, g_post_ffn)


def setup_inputs(seed: int = 0) -> dict:
    key = jax.random.key(seed)
    ks = jax.random.split(key, 24)
    nrm = lambda k, shape, s: jax.random.normal(k, shape, jnp.float32) * s
    L, D = DEPTH, D_MODEL
    return {
        'x_prompt': nrm(ks[0], (BATCH, SEQ, D), 1.0),
        'x_sample': nrm(ks[1], (DEC_BATCH, DEC_SEQ, D), 1.0),
        'c_prompt': nrm(ks[2], (BATCH, D), 1.0),
        'c_sample': nrm(ks[3], (DEC_BATCH, D), 1.0),
        'w_ada': nrm(ks[4], (L, D, N_MOD * D), 0.5 * D ** -0.5),
        'b_ada': nrm(ks[5], (L, N_MOD * D), 0.01),
        'g_pre_mix': 1.0 + nrm(ks[6], (L, D), 0.02),
        'g_post_mix': 1.0 + nrm(ks[7], (L, D), 0.02),
        'w_in': nrm(ks[8], (L, D, N_IN), D ** -0.5),
        'w_gla_up_fwd': nrm(ks[9], (L, GLA_RANK, GLA_DK_TOTAL), GLA_RANK ** -0.5),
        'b_gla_fwd': nrm(ks[10], (L, GLA_DK_TOTAL), 0.01),
        'w_gla_up_bwd': nrm(ks[11], (L, GLA_RANK, GLA_DK_TOTAL), GLA_RANK ** -0.5),
        'b_gla_bwd': nrm(ks[12], (L, GLA_DK_TOTAL), 0.01),
        'g_gla_norm': 1.0 + nrm(ks[13], (L, GLA_DV_TOTAL), 0.02),
        'w_branch_a': nrm(ks[14], (L, ATTN_OUT, D), ATTN_OUT ** -0.5),
        'w_branch_b': nrm(ks[15], (L, GLA_DV_TOTAL, D), GLA_DV_TOTAL ** -0.5),
        'w_out': nrm(ks[16], (L, D, D), D ** -0.5),
        'g_pre_ffn': 1.0 + nrm(ks[17], (L, D), 0.02),
        'g_post_ffn': 1.0 + nrm(ks[18], (L, D), 0.02),
        'w_peer_q': nrm(ks[19], (L, D, PEER_HEADS * PEER_QDIM), D ** -0.5),
        'peer_subkeys': nrm(ks[20], (L, 2, PEER_KEYS, PEER_HALF), PEER_HALF ** -0.5),
        'peer_u': nrm(ks[21], (L, PEER_EXPERTS, D), D ** -0.5),
        'peer_v': nrm(ks[22], (L, PEER_EXPERTS, D), D ** -0.5),
    }


def reference(x_prompt, x_sample, c_prompt, c_sample, w_ada, b_ada, g_pre_mix, g_post_mix, w_in,
              w_gla_up_fwd, b_gla_fwd, w_gla_up_bwd, b_gla_bwd, g_gla_norm, w_branch_a, w_branch_b,
              w_out, g_pre_ffn, g_post_ffn, w_peer_q, peer_subkeys, peer_u, peer_v):
    y_prompt, y_sample = x_prompt, x_sample
    for l in range(DEPTH):
        params = (w_ada[l], b_ada[l], g_pre_mix[l], g_post_mix[l], w_in[l], w_gla_up_fwd[l], b_gla_fwd[l],
                  w_gla_up_bwd[l], b_gla_bwd[l], g_gla_norm[l], w_branch_a[l], w_branch_b[l], w_out[l],
                  g_pre_ffn[l], g_post_ffn[l], w_peer_q[l], peer_subkeys[l], peer_u[l], peer_v[l])
        y_prompt = encoder_layer(y_prompt, c_prompt, *params)
        y_sample = encoder_layer(y_sample, c_sample, *params)
    return (y_prompt, y_sample)
```

```python
import functools
import math

import jax
import jax.numpy as jnp
from jax import lax
from jax.experimental import pallas as pl
from jax.experimental.pallas import tpu as pltpu

F32 = jnp.float32
BF16 = jnp.bfloat16
HIGHEST = lax.Precision.HIGHEST

D_MODEL = 1024
N_MOD = 6
RMS_EPS = 1e-6
NEG_INF = -1e30

ATTN_HEAD_DIM = 128
ATTN_HEADS = 4
DILATIONS = (1, 4, 16)
ATTN_HALF = 64
ATTN_GROUP_W = ATTN_HEADS * ATTN_HEAD_DIM
ATTN_W = len(DILATIONS) * ATTN_GROUP_W
ROPE_THETA = 10000.0
ATTN_TQ = {1: 512, 4: 1024, 16: 2048}
ATTN_QB = 128

GLA_HEADS = 4
GLA_DK = 128
GLA_DV = 256
GLA_RANK = 16
GLA_TAU = 16.0
GLA_CHUNK = 64
GLA_TB = 512
GLA_LR_PAD = 128

PEER_HEADS = 8
PEER_KEYS = 128
PEER_HALF = 128
PEER_TOPK = 16
PEER_QW = PEER_HEADS * 2 * PEER_HALF
PEER_TT = 512
PEER_EC = 1024
PEER_STRIP = 128

PROJ_TM = 512
MIX_TM = 256
VMEM_LIMIT = 56 * 1024 * 1024


def _cparams(sem):
    return pltpu.CompilerParams(dimension_semantics=sem, vmem_limit_bytes=VMEM_LIMIT)


def _rms(x, g):
    return x * lax.rsqrt(jnp.mean(x * x, axis=-1, keepdims=True) + RMS_EPS) * g


def _silu(x):
    return x * jax.nn.sigmoid(x)


def _mod_kernel(c_ref, w_ref, b_ref, o_ref):
    o_ref[...] = jnp.dot(_silu(c_ref[...]), w_ref[...], preferred_element_type=F32,
                         precision=HIGHEST) + b_ref[...]


def _mod(c, w_ada, b_ada):
    rows = c.shape[0]
    tn = 1536
    return pl.pallas_call(
        _mod_kernel,
        grid=(N_MOD * D_MODEL // tn,),
        in_specs=[pl.BlockSpec((rows, D_MODEL), lambda j: (0, 0)),
                  pl.BlockSpec((D_MODEL, tn), lambda j: (0, j)),
                  pl.BlockSpec((1, tn), lambda j: (0, j))],
        out_specs=pl.BlockSpec((rows, tn), lambda j: (0, j)),
        out_shape=jax.ShapeDtypeStruct((rows, N_MOD * D_MODEL), F32),
        compiler_params=_cparams(("arbitrary",)),
        name="mod",
    )(c, w_ada, b_ada.reshape(1, -1))


def _proj_attn_kernel(x_ref, g_ref, sc_ref, sh_ref, w_ref, cos_ref, sin_ref, o_ref, h_scr):
    j = pl.program_id(2)

    @pl.when(j == 0)
    def _():
        h = _rms(x_ref[...], g_ref[...]) * (1.0 + sc_ref[...]) + sh_ref[...]
        h_scr[...] = h.astype(BF16)

    r = jnp.dot(h_scr[...], w_ref[...], preferred_element_type=F32)

    @pl.when(j == 2)
    def _():
        o_ref[...] = r

    @pl.when(j < 2)
    def _():
        cos = cos_ref[...]
        sin = sin_ref[...]
        scale = jnp.where(j == 0, ATTN_HEAD_DIM ** -0.5, 1.0).astype(F32)
        for hh in range(ATTN_W // ATTN_HEAD_DIM):
            t = r[:, hh * ATTN_HEAD_DIM:(hh + 1) * ATTN_HEAD_DIM]
            rot = pltpu.roll(t, ATTN_HEAD_DIM // 2, axis=1)
            o_ref[:, hh * ATTN_HEAD_DIM:(hh + 1) * ATTN_HEAD_DIM] = (t * cos + rot * sin) * scale


def _proj_attn(x, g, sc, sh, w_qkv, cos2, sin2):
    B, S, _ = x.shape
    tm = PROJ_TM
    vec = pl.BlockSpec((None, 1, D_MODEL), lambda b, i, j: (b, 0, 0))
    return pl.pallas_call(
        _proj_attn_kernel,
        grid=(B, S // tm, 3),
        in_specs=[pl.BlockSpec((None, tm, D_MODEL), lambda b, i, j: (b, i, 0)),
                  pl.BlockSpec((1, D_MODEL), lambda b, i, j: (0, 0)),
                  vec, vec,
                  pl.BlockSpec((D_MODEL, ATTN_W), lambda b, i, j: (0, j)),
                  pl.BlockSpec((tm, ATTN_HEAD_DIM), lambda b, i, j: (i, 0)),
                  pl.BlockSpec((tm, ATTN_HEAD_DIM), lambda b, i, j: (i, 0))],
        out_specs=pl.BlockSpec((None, tm, ATTN_W), lambda b, i, j: (b, i, j)),
        out_shape=jax.ShapeDtypeStruct((B, S, 3 * ATTN_W), F32),
        scratch_shapes=[pltpu.VMEM((tm, D_MODEL), BF16)],
        compiler_params=_cparams(("parallel", "parallel", "arbitrary")),
        name="proj_attn",
    )(x, g, sc, sh, w_qkv, cos2, sin2)


GLA_QK_W = GLA_HEADS * GLA_DK
GLA_V_W = GLA_HEADS * GLA_DV
REST_TN = 1024
REST_W = 2 * GLA_QK_W + 2 * GLA_V_W + 2 * D_MODEL


def _proj_rest_kernel(x_ref, g_ref, sc_ref, sh_ref, w_ref, wlr_ref, o_ref, olr_ref, h_scr):
    j = pl.program_id(2)

    @pl.when(j == 0)
    def _():
        h = _rms(x_ref[...], g_ref[...]) * (1.0 + sc_ref[...]) + sh_ref[...]
        h_scr[...] = h.astype(BF16)

    r = jnp.dot(h_scr[...], w_ref[...], preferred_element_type=F32)

    @pl.when(j == 0)
    def _():
        o_ref[:, :GLA_QK_W] = (r[:, :GLA_QK_W] * GLA_DK ** -0.5).astype(BF16)
        o_ref[:, GLA_QK_W:] = r[:, GLA_QK_W:].astype(BF16)
        olr_ref[...] = jnp.dot(h_scr[...], wlr_ref[...], preferred_element_type=F32)

    @pl.when(j > 0)
    def _():
        o_ref[...] = r.astype(BF16)


def _proj_rest(x, g, sc, sh, w_rest, w_lr):
    B, S, _ = x.shape
    tm = PROJ_TM
    vec = pl.BlockSpec((None, 1, D_MODEL), lambda b, i, j: (b, 0, 0))
    return pl.pallas_call(
        _proj_rest_kernel,
        grid=(B, S // tm, REST_W // REST_TN),
        in_specs=[pl.BlockSpec((None, tm, D_MODEL), lambda b, i, j: (b, i, 0)),
                  pl.BlockSpec((1, D_MODEL), lambda b, i, j: (0, 0)),
                  vec, vec,
                  pl.BlockSpec((D_MODEL, REST_TN), lambda b, i, j: (0, j)),
                  pl.BlockSpec((D_MODEL, GLA_LR_PAD), lambda b, i, j: (0, 0))],
        out_specs=[pl.BlockSpec((None, tm, REST_TN), lambda b, i, j: (b, i, j)),
                   pl.BlockSpec((None, tm, GLA_LR_PAD), lambda b, i, j: (b, i, 0))],
        out_shape=[jax.ShapeDtypeStruct((B, S, REST_W), BF16),
                   jax.ShapeDtypeStruct((B, S, GLA_LR_PAD), F32)],
        scratch_shapes=[pltpu.VMEM((tm, D_MODEL), BF16)],
        compiler_params=_cparams(("parallel", "parallel", "arbitrary")),
        name="proj_rest",
    )(x, g, sc, sh, w_rest, w_lr)


def _attn_kernel(q_ref, kp_ref, kc_ref, kn_ref, vp_ref, vc_ref, vn_ref, o_ref, lse_ref,
                 o_scr, l_scr, *, d, tq, seq):
    n = pl.program_id(1)
    h = pl.program_id(2)
    L = ATTN_HALF
    mc = tq // d
    qb = ATTN_QB
    nsub = mc // qb
    kw = qb + 2 * L
    qq = lax.broadcasted_iota(jnp.int32, (qb, kw), 0)
    kk = lax.broadcasted_iota(jnp.int32, (qb, kw), 1)
    band = jnp.abs(kk - qq - L) <= L
    kk_row = lax.broadcasted_iota(jnp.int32, (1, kw), 1)
    p0 = n * tq

    def rows(ref, r, c0, cnt):
        if d == 1:
            return ref[pl.ds(c0, cnt), :]
        return ref[pl.ds(r + c0 * d, cnt, stride=d), :]

    def residue(r):
        for j in range(nsub):
            lo = j * qb - L
            qs = rows(q_ref, r, j * qb, qb)
            kparts, vparts = [], []
            if lo < 0:
                kparts.append(rows(kp_ref, r, mc + lo, -lo))
                vparts.append(rows(vp_ref, r, mc + lo, -lo))
            c0, c1 = max(lo, 0), min(lo + kw, mc)
            kparts.append(rows(kc_ref, r, c0, c1 - c0))
            vparts.append(rows(vc_ref, r, c0, c1 - c0))
            if lo + kw > mc:
                kparts.append(rows(kn_ref, r, 0, lo + kw - mc))
                vparts.append(rows(vn_ref, r, 0, lo + kw - mc))
            ks = jnp.concatenate(kparts, axis=0).astype(BF16)
            vs = jnp.concatenate(vparts, axis=0).astype(BF16)
            s = lax.dot_general(qs.astype(BF16), ks, (((1,), (1,)), ((), ())),
                                preferred_element_type=F32)
            pos = p0 + (lo + kk_row) * d + r
            s = jnp.where(band, s, NEG_INF)
            s = jnp.where(pos >= 0, s, NEG_INF)
            s = jnp.where(pos < seq, s, NEG_INF)
            m = jnp.max(s, axis=-1, keepdims=True)
            p = jnp.exp(s - m)
            l = jnp.sum(p, axis=-1, keepdims=True)
            o = jnp.dot(p.astype(BF16), vs, preferred_element_type=F32) / l
            lse = jnp.broadcast_to(m + jnp.log(l), (qb, ATTN_HEAD_DIM))
            if d == 1:
                o_scr[pl.ds(j * qb, qb), :] = o
                l_scr[pl.ds(j * qb, qb), :] = lse
            else:
                o_scr[pl.ds(r + j * qb * d, qb, stride=d), :] = o
                l_scr[pl.ds(r + j * qb * d, qb, stride=d), :] = lse

    if d == 1:
        residue(0)
    else:
        def body(r, carry):
            residue(r)
            return carry
        lax.fori_loop(0, d, body, 0)

    o_ref[...] = o_scr[...].astype(BF16)
    lane = lax.broadcasted_iota(jnp.int32, (tq, ATTN_HEAD_DIM), 1)

    @pl.when(h == 0)
    def _():
        lse_ref[...] = jnp.where(lane == 0, l_scr[...], 0.0)

    @pl.when(h > 0)
    def _():
        lse_ref[...] = jnp.where(lane == h, l_scr[...], lse_ref[...])


def _attn_group(qkv, gi, d):
    B, S, _ = qkv.shape
    tq = ATTN_TQ[d]
    nt = S // tq
    hb = ATTN_W // ATTN_HEAD_DIM

    def col(sec):
        return lambda b, n, h: sec * hb + gi * ATTN_HEADS + h

    def spec(sec, shift):
        c = col(sec)
        if shift == 0:
            return pl.BlockSpec((None, tq, ATTN_HEAD_DIM), lambda b, n, h: (b, n, c(b, n, h)))
        if shift < 0:
            return pl.BlockSpec((None, tq, ATTN_HEAD_DIM),
                                lambda b, n, h: (b, jnp.maximum(n - 1, 0), c(b, n, h)))
        return pl.BlockSpec((None, tq, ATTN_HEAD_DIM),
                            lambda b, n, h: (b, jnp.minimum(n + 1, nt - 1), c(b, n, h)))

    return pl.pallas_call(
        functools.partial(_attn_kernel, d=d, tq=tq, seq=S),
        grid=(B, nt, ATTN_HEADS),
        in_specs=[spec(0, 0), spec(1, -1), spec(1, 0), spec(1, 1),
                  spec(2, -1), spec(2, 0), spec(2, 1)],
        out_specs=[pl.BlockSpec((None, tq, ATTN_HEAD_DIM), lambda b, n, h: (b, n, h)),
                   pl.BlockSpec((None, tq, ATTN_HEAD_DIM), lambda b, n, h: (b, n, 0))],
        out_shape=[jax.ShapeDtypeStruct((B, S, ATTN_GROUP_W), BF16),
                   jax.ShapeDtypeStruct((B, S, ATTN_HEAD_DIM), F32)],
        scratch_shapes=[pltpu.VMEM((tq, ATTN_HEAD_DIM), F32),
                        pltpu.VMEM((tq, ATTN_HEAD_DIM), F32)],
        compiler_params=_cparams(("parallel", "parallel", "arbitrary")),
        name=f"attn_d{d}",
    )(qkv, qkv, qkv, qkv, qkv, qkv, qkv)


def _gla_unit(q_ref, k_ref, v_ref, la_ref, r0, h, cum, mask, st_ref, o_ref, last):
    C = GLA_CHUNK
    ks = slice(h * GLA_DK, (h + 1) * GLA_DK)
    vs = slice(h * GLA_DV, (h + 1) * GLA_DV)
    q = q_ref[pl.ds(r0, C), ks].astype(F32)
    k = k_ref[pl.ds(r0, C), ks].astype(F32)
    v = v_ref[pl.ds(r0, C), vs]
    la = la_ref[pl.ds(r0, C), ks]
    b = jnp.dot(cum, la, preferred_element_type=F32, precision=HIGHEST)
    bl = b[last:last + 1, :]
    qd = (q * jnp.exp(b)).astype(BF16)
    kd = (k * jnp.exp(-b)).astype(BF16)
    kl = (k * jnp.exp(bl - b)).astype(BF16)
    att = lax.dot_general(qd, kd, (((1,), (1,)), ((), ())), preferred_element_type=F32)
    att = jnp.where(mask, att, 0.0)
    st = st_ref[h]
    o = (jnp.dot(att.astype(BF16), v, preferred_element_type=F32)
         + jnp.dot(qd, st.astype(BF16), preferred_element_type=F32))
    o_ref[pl.ds(r0, C), vs] = o.astype(BF16)
    dec = jnp.transpose(jnp.broadcast_to(jnp.exp(bl), (GLA_DK, GLA_DK)))
    upd = lax.dot_general(kl, v, (((0,), (0,)), ((), ())), preferred_element_type=F32)
    st_ref[h] = jnp.concatenate([dec, dec], axis=1) * st + upd


def _gla_kernel(qf_ref, kf_ref, vf_ref, lrf_ref, qb_ref, kb_ref, vb_ref, lrb_ref,
                wuf_ref, bf_ref, wub_ref, bb_ref, of_ref, ob_ref,
                sf_ref, sb_ref, laf_scr, lab_scr):
    n = pl.program_id(1)
    C = GLA_CHUNK
    nch = GLA_TB // C

    @pl.when(n == 0)
    def _():
        sf_ref[...] = jnp.zeros_like(sf_ref)
        sb_ref[...] = jnp.zeros_like(sb_ref)

    zf = jnp.dot(lrf_ref[...], wuf_ref[...], preferred_element_type=F32, precision=HIGHEST) + bf_ref[...]
    zb = jnp.dot(lrb_ref[...], wub_ref[...], preferred_element_type=F32, precision=HIGHEST) + bb_ref[...]
    laf_scr[...] = jax.nn.log_sigmoid(zf) / GLA_TAU
    lab_scr[...] = jax.nn.log_sigmoid(zb) / GLA_TAU

    row = lax.broadcasted_iota(jnp.int32, (C, C), 0)
    colm = lax.broadcasted_iota(jnp.int32, (C, C), 1)
    tril = row >= colm
    triu = row <= colm
    cum_f = jnp.where(tril, 1.0, 0.0).astype(F32)
    cum_b = jnp.where(triu, 1.0, 0.0).astype(F32)

    def chunk(c, carry):
        rf = pl.multiple_of(c * C, C)
        rb = pl.multiple_of((nch - 1 - c) * C, C)
        for h in range(GLA_HEADS):
            _gla_unit(qf_ref, kf_ref, vf_ref, laf_scr, rf, h, cum_f, tril, sf_ref, of_ref, C - 1)
            _gla_unit(qb_ref, kb_ref, vb_ref, lab_scr, rb, h, cum_b, triu, sb_ref, ob_ref, 0)
        return carry

    lax.fori_loop(0, nch, chunk, 0)


def _gla(rest, lr, wuf, bf, wub, bb):
    B, S, _ = rest.shape
    tb = GLA_TB
    nb = S // tb
    fwd = lambda b, n: (b, n, 0)
    bwd = lambda b, n: (b, nb - 1 - n, 0)
    const = lambda b, n: (0, 0)

    def qkv_specs(idx):
        blk = lambda c: (lambda b, n: idx(b, n)[:2] + (c,))
        return [pl.BlockSpec((None, tb, GLA_QK_W), blk(0)),
                pl.BlockSpec((None, tb, GLA_QK_W), blk(1)),
                pl.BlockSpec((None, tb, GLA_V_W), blk(1)),
                pl.BlockSpec((None, tb, GLA_LR_PAD), idx)]

    return pl.pallas_call(
        _gla_kernel,
        grid=(B, nb),
        in_specs=qkv_specs(fwd) + qkv_specs(bwd) + [
            pl.BlockSpec((GLA_LR_PAD, GLA_QK_W), const), pl.BlockSpec((1, GLA_QK_W), const),
            pl.BlockSpec((GLA_LR_PAD, GLA_QK_W), const), pl.BlockSpec((1, GLA_QK_W), const)],
        out_specs=[pl.BlockSpec((None, tb, GLA_V_W), fwd),
                   pl.BlockSpec((None, tb, GLA_V_W), bwd)],
        out_shape=[jax.ShapeDtypeStruct((B, S, GLA_V_W), BF16),
                   jax.ShapeDtypeStruct((B, S, GLA_V_W), BF16)],
        scratch_shapes=[pltpu.VMEM((GLA_HEADS, GLA_DK, GLA_DV), F32),
                        pltpu.VMEM((GLA_HEADS, GLA_DK, GLA_DV), F32),
                        pltpu.VMEM((tb, GLA_QK_W), F32),
                        pltpu.VMEM((tb, GLA_QK_W), F32)],
        compiler_params=_cparams(("parallel", "arbitrary")),
        name="gla",
    )(rest, rest, rest, lr, rest, rest, rest, lr, wuf, bf, wub, bb)


def _mix_kernel(x_ref, o1_ref, o4_ref, o16_ref, l1_ref, l4_ref, l16_ref, of_ref, ob_ref,
                gr_ref, ma_ref, mb_ref, gt1_ref, sc2_ref, sh2_ref, ggla_ref, gpost_ref, gpre_ref,
                wa_ref, wb_ref, wo_ref, wq_ref, x1_ref, h2_ref, qp_ref):
    lses = [l1_ref[...], l4_ref[...], l16_ref[...]]
    outs = [o1_ref, o4_ref, o16_ref]
    mx = jnp.maximum(jnp.maximum(lses[0], lses[1]), lses[2])
    es = [jnp.exp(l - mx) for l in lses]
    den = es[0] + es[1] + es[2]
    ws = [e / den for e in es]
    heads = []
    for h in range(ATTN_HEADS):
        hs = slice(h * ATTN_HEAD_DIM, (h + 1) * ATTN_HEAD_DIM)
        acc = ws[0][:, h:h + 1] * outs[0][:, hs].astype(F32)
        acc += ws[1][:, h:h + 1] * outs[1][:, hs].astype(F32)
        acc += ws[2][:, h:h + 1] * outs[2][:, hs].astype(F32)
        heads.append(acc)
    a_out = jnp.concatenate(heads, axis=1).astype(BF16)

    gparts = []
    for h in range(GLA_HEADS):
        vs = slice(h * GLA_DV, (h + 1) * GLA_DV)
        gs = of_ref[:, vs].astype(F32) + ob_ref[:, vs].astype(F32)
        gparts.append(_rms(gs, ggla_ref[:, vs]))
    go = (jnp.concatenate(gparts, axis=1) * _silu(gr_ref[...].astype(F32))).astype(BF16)

    merged = (jax.nn.sigmoid(ma_ref[...].astype(F32)) * jnp.dot(a_out, wa_ref[...], preferred_element_type=F32)
              + jax.nn.sigmoid(mb_ref[...].astype(F32)) * jnp.dot(go, wb_ref[...], preferred_element_type=F32))
    z = jnp.dot(merged.astype(BF16), wo_ref[...], preferred_element_type=F32)
    x1 = x_ref[...] + gt1_ref[...] * _rms(z, gpost_ref[...])
    x1_ref[...] = x1
    h2 = (_rms(x1, gpre_ref[...]) * (1.0 + sc2_ref[...]) + sh2_ref[...]).astype(BF16)
    h2_ref[...] = h2
    qp_ref[...] = jnp.dot(h2, wq_ref[...], preferred_element_type=F32)


def _mix(x, o_groups, l_groups, of, ob, rest, gt1, sc2, sh2, ggla, gpost, gpre, wa, wb, wo, wq):
    B, S, _ = x.shape
    tm = MIX_TM
    tok = lambda w: pl.BlockSpec((None, tm, w), lambda b, i: (b, i, 0))
    restblk = lambda c: pl.BlockSpec((None, tm, D_MODEL), lambda b, i: (b, i, c))
    vec = pl.BlockSpec((None, 1, D_MODEL), lambda b, i: (b, 0, 0))
    gain = pl.BlockSpec((1, D_MODEL), lambda b, i: (0, 0))
    full = lambda a: pl.BlockSpec(a.shape, lambda b, i: (0, 0))
    return pl.pallas_call(
        _mix_kernel,
        grid=(B, S // tm),
        in_specs=[tok(D_MODEL)] + [tok(ATTN_GROUP_W)] * 3 + [tok(ATTN_HEAD_DIM)] * 3
                 + [tok(GLA_V_W)] * 2 + [restblk(2), restblk(3), restblk(4)]
                 + [vec] * 3 + [gain] * 3 + [full(wa), full(wb), full(wo), full(wq)],
        out_specs=[tok(D_MODEL), tok(D_MODEL), tok(PEER_QW)],
        out_shape=[jax.ShapeDtypeStruct((B, S, D_MODEL), F32),
                   jax.ShapeDtypeStruct((B, S, D_MODEL), BF16),
                   jax.ShapeDtypeStruct((B, S, PEER_QW), F32)],
        compiler_params=_cparams(("parallel", "parallel")),
        name="mix",
    )(x, *o_groups, *l_groups, of, ob, rest, rest, rest, gt1, sc2, sh2, ggla, gpost, gpre, wa, wb, wo, wq)


def _peer_cand_pieces(tv1, tv2):
    pieces = [tv1[0:1] + tv2[0:8], tv1[0:1] + tv2[8:16]]
    for a in range(1, 8):
        pieces.append(tv1[a:a + 1] + tv2[0:8])
    pieces.append(tv1[8:16] + tv2[0:1])
    return jnp.concatenate(pieces, axis=0)


def _top_rows(cur, k, sink):
    for r in range(k):
        m = jnp.max(cur, axis=0, keepdims=True)
        sink[r:r + 1, :] = m
        if r + 1 < k:
            cur = jnp.where(cur == m, -jnp.inf, cur)


def _kth_largest(cur, k):
    cum = thr = z = mx = None
    for r in range(k):
        m = jnp.max(cur, axis=0, keepdims=True)
        eq = cur == m
        n = jnp.sum(jnp.where(eq, 1.0, 0.0), axis=0, keepdims=True)
        if r == 0:
            mx, thr, z, cum = m, m, n, n
        else:
            live = cum < k
            thr = jnp.where(live, m, thr)
            z = z + jnp.where(live, n * jnp.exp(m - mx), 0.0)
            cum = cum + n
        if r + 1 < k:
            cur = jnp.where(eq, -jnp.inf, cur)
    return thr, z


def _peer_kernel(qp_ref, h2_ref, x1_ref, gt2_ref, gpost_ref, sk_ref, u_ref, vt_ref, y_ref,
                 h2t_scr, acc_scr, at_scr, act_scr, s1_scr, s2_scr, c1_scr, e2_scr, thr_scr, tv_scr):
    j = pl.program_id(2)
    nj = pl.num_programs(2)
    tt = PEER_TT

    @pl.when(j == 0)
    def _():
        h2t_scr[...] = h2_ref[...].T
        acc_scr[...] = jnp.zeros_like(acc_scr)
        for h in range(PEER_HEADS):
            for p in range(2):
                qs = qp_ref[:, (2 * h + p) * PEER_HALF:(2 * h + p + 1) * PEER_HALF]
                st = lax.dot_general(sk_ref[p], qs, (((1,), (1,)), ((), ())),
                                     preferred_element_type=F32, precision=HIGHEST)
                (s1_scr if p == 0 else s2_scr)[h] = st
                _top_rows(st, PEER_TOPK, sink=tv_scr.at[p])
            tv1 = tv_scr[0]
            tv2 = tv_scr[1]
            thr, z = _kth_largest(_peer_cand_pieces(tv1, tv2), PEER_TOPK)
            thr_scr[h:h + 1, :] = thr
            c1_scr[h] = jnp.exp(s1_scr[h] - tv1[0:1]) / z
            e2_scr[h] = jnp.exp(s2_scr[h] - tv2[0:1])

    at_scr[...] = jnp.dot(u_ref[...], h2t_scr[...], preferred_element_type=F32)

    def per_i1(ii, carry):
        i1 = j * (PEER_EC // PEER_KEYS) + ii
        e0 = pl.multiple_of(ii * PEER_KEYS, PEER_KEYS)
        s1rows = [s1_scr[h, pl.ds(i1, 1), :] for h in range(PEER_HEADS)]
        c1rows = [c1_scr[h, pl.ds(i1, 1), :] for h in range(PEER_HEADS)]
        for s in range(tt // PEER_STRIP):
            ls = slice(s * PEER_STRIP, (s + 1) * PEER_STRIP)
            w = jnp.zeros((PEER_KEYS, PEER_STRIP), F32)
            for h in range(PEER_HEADS):
                sel = (s1rows[h][:, ls] + s2_scr[h, :, ls]) >= thr_scr[h:h + 1, ls]
                w = w + jnp.where(sel, e2_scr[h, :, ls], 0.0) * c1rows[h][:, ls]
            a = at_scr[pl.ds(e0, PEER_KEYS), ls]
            gelu = 0.5 * a * (1.0 + lax.erf(a * (1.0 / math.sqrt(2.0))))
            act_scr[pl.ds(e0, PEER_KEYS), ls] = (gelu * w).astype(BF16)
        return carry

    lax.fori_loop(0, PEER_EC // PEER_KEYS, per_i1, 0)
    acc_scr[...] += jnp.dot(vt_ref[...], act_scr[...], preferred_element_type=F32)

    @pl.when(j == nj - 1)
    def _():
        f = acc_scr[...].T
        y_ref[...] = x1_ref[...] + gt2_ref[...] * _rms(f, gpost_ref[...])


def _peer(qp, h2, x1, gt2, gpost, subkeys, u_bf, vt_bf):
    B, S, _ = x1.shape
    tt = PEER_TT
    ne = u_bf.shape[0]
    tok = lambda w: pl.BlockSpec((None, tt, w), lambda b, i, j: (b, i, 0))
    return pl.pallas_call(
        _peer_kernel,
        grid=(B, S // tt, ne // PEER_EC),
        in_specs=[tok(PEER_QW), tok(D_MODEL), tok(D_MODEL),
                  pl.BlockSpec((None, 1, D_MODEL), lambda b, i, j: (b, 0, 0)),
                  pl.BlockSpec((1, D_MODEL), lambda b, i, j: (0, 0)),
                  pl.BlockSpec((2, PEER_KEYS, PEER_HALF), lambda b, i, j: (0, 0, 0)),
                  pl.BlockSpec((PEER_EC, D_MODEL), lambda b, i, j: (j, 0)),
                  pl.BlockSpec((D_MODEL, PEER_EC), lambda b, i, j: (0, j))],
        out_specs=tok(D_MODEL),
        out_shape=jax.ShapeDtypeStruct((B, S, D_MODEL), F32),
        scratch_shapes=[pltpu.VMEM((D_MODEL, tt), BF16),
                        pltpu.VMEM((D_MODEL, tt), F32),
                        pltpu.VMEM((PEER_EC, tt), F32),
                        pltpu.VMEM((PEER_EC, tt), BF16),
                        pltpu.VMEM((PEER_HEADS, PEER_KEYS, tt), F32),
                        pltpu.VMEM((PEER_HEADS, PEER_KEYS, tt), F32),
                        pltpu.VMEM((PEER_HEADS, PEER_KEYS, tt), F32),
                        pltpu.VMEM((PEER_HEADS, PEER_KEYS, tt), F32),
                        pltpu.VMEM((PEER_HEADS, tt), F32),
                        pltpu.VMEM((2, PEER_TOPK, tt), F32)],
        compiler_params=_cparams(("parallel", "parallel", "arbitrary")),
        name="peer",
    )(qp, h2, x1, gt2, gpost, subkeys, u_bf, vt_bf)


def _rope_tables(S):
    half = ATTN_HEAD_DIM // 2
    inv = ROPE_THETA ** (-jnp.arange(half, dtype=F32) / half)
    ang = jnp.arange(S, dtype=F32)[:, None] * inv[None, :]
    cos, sin = jnp.cos(ang), jnp.sin(ang)
    return jnp.concatenate([cos, cos], axis=1), jnp.concatenate([-sin, sin], axis=1)


def _prep_weights(w_in, w_gla_up_fwd, b_gla_fwd, w_gla_up_bwd, b_gla_bwd, w_branch_a, w_branch_b,
                  w_out, w_peer_q, peer_subkeys, peer_u, peer_v):
    a_end = 3 * ATTN_W
    g_end = a_end + 2 * GLA_QK_W + 2 * GLA_V_W
    lr_end = g_end + 2 * GLA_RANK
    w_qkv = w_in[:, :a_end].astype(BF16)
    w_rest = jnp.concatenate([w_in[:, a_end:g_end], w_in[:, lr_end:]], axis=1).astype(BF16)
    w_lr = jnp.pad(w_in[:, g_end:lr_end], ((0, 0), (0, GLA_LR_PAD - 2 * GLA_RANK))).astype(BF16)
    wuf = jnp.pad(w_gla_up_fwd, ((0, GLA_LR_PAD - GLA_RANK), (0, 0)))
    wub = jnp.pad(w_gla_up_bwd, ((GLA_RANK, GLA_LR_PAD - 2 * GLA_RANK), (0, 0)))
    return dict(
        w_qkv=w_qkv, w_rest=w_rest, w_lr=w_lr, wuf=wuf, wub=wub,
        bf=b_gla_fwd.reshape(1, -1), bb=b_gla_bwd.reshape(1, -1),
        wa=w_branch_a.astype(BF16), wb=w_branch_b.astype(BF16), wo=w_out.astype(BF16),
        wq=w_peer_q.astype(BF16), subkeys=peer_subkeys,
        u=peer_u.astype(BF16), vt=peer_v.T.astype(BF16))


def _encoder_layer(x, mod, gains, wp):
    B, S, _ = x.shape
    sh1, sc1, gt1, sh2, sc2, gt2 = [mod[:, None, k * D_MODEL:(k + 1) * D_MODEL] for k in range(N_MOD)]
    g_pre_mix, g_post_mix, g_gla_norm, g_pre_ffn, g_post_ffn = gains
    cos2, sin2 = _rope_tables(S)
    qkv = _proj_attn(x, g_pre_mix, sc1, sh1, wp["w_qkv"], cos2, sin2)
    rest, lr = _proj_rest(x, g_pre_mix, sc1, sh1, wp["w_rest"], wp["w_lr"])
    o_groups, l_groups = [], []
    for gi, d in enumerate(DILATIONS):
        o, l = _attn_group(qkv, gi, d)
        o_groups.append(o)
        l_groups.append(l)
    of, ob = _gla(rest, lr, wp["wuf"], wp["bf"], wp["wub"], wp["bb"])
    x1, h2, qp = _mix(x, o_groups, l_groups, of, ob, rest, gt1, sc2, sh2, g_gla_norm, g_post_mix,
                      g_pre_ffn, wp["wa"], wp["wb"], wp["wo"], wp["wq"])
    return _peer(qp, h2, x1, gt2, g_post_ffn, wp["subkeys"], wp["u"], wp["vt"])


def kernel(x_prompt, x_sample, c_prompt, c_sample, w_ada, b_ada, g_pre_mix, g_post_mix, w_in, w_gla_up_fwd, b_gla_fwd, w_gla_up_bwd, b_gla_bwd, g_gla_norm, w_branch_a, w_branch_b, w_out, g_pre_ffn, g_post_ffn, w_peer_q, peer_subkeys, peer_u, peer_v):
    depth = w_ada.shape[0]
    nb_p = c_prompt.shape[0]
    y_p, y_s = x_prompt, x_sample
    c_all = jnp.concatenate([c_prompt, c_sample], axis=0)
    pad = (-c_all.shape[0]) % 8
    c_all = jnp.pad(c_all, ((0, pad), (0, 0)))
    for l in range(depth):
        mod = _mod(c_all, w_ada[l], b_ada[l])
        gains = tuple(g[l].reshape(1, -1) for g in (g_pre_mix, g_post_mix, g_gla_norm, g_pre_ffn, g_post_ffn))
        wp = _prep_weights(w_in[l], w_gla_up_fwd[l], b_gla_fwd[l], w_gla_up_bwd[l], b_gla_bwd[l],
                           w_branch_a[l], w_branch_b[l], w_out[l], w_peer_q[l], peer_subkeys[l],
                           peer_u[l], peer_v[l])
        y_p = _encoder_layer(y_p, mod[:nb_p], gains, wp)
        y_s = _encoder_layer(y_s, mod[nb_p:nb_p + c_sample.shape[0]], gains, wp)
    return (y_p, y_s)
```
